```python
import functools
import jax, jax.numpy as jnp
from jax import lax
import numpy as np

D_MODEL = 1024
BATCH = 1
SEQ = 16384
DEPTH = 1
DEC_BATCH = 128
DEC_SEQ = 4
PAST_LEN = 8192
PAGE_SIZE = 128

HEAD_DIM = 64
N_HEADS = D_MODEL // HEAD_DIM
H_FOX = N_HEADS // 2
H_SB = N_HEADS - H_FOX
MIX_WIDTH = N_HEADS * HEAD_DIM
Q_BLOCK = 128
ATTN_SCALE = HEAD_DIM ** -0.5
N_EXPERTS = 64
TOP_K = 6
N_GROUPS = 8
TOPK_GROUPS = 4
EXPERT_FF = 256
SHARED_FF = 256
ROUTED_SCALE = 2.5
N_MOD = 6
EPS = 1e-6

kernel_name = "fox_stickbreaking_hymba_moe_adaln_step"


def _rmsnorm(x, g):
    x32 = x.astype(jnp.float32)
    y = x32 * lax.rsqrt(jnp.mean(x32 * x32, axis=-1, keepdims=True) + EPS)
    return (y * g.astype(jnp.float32)).astype(x.dtype)


def _head_weights(z, fq, fk, q_pos, k_pos):
    causal = k_pos[None, :] <= q_pos[:, None]
    strict = k_pos[None, :] < q_pos[:, None]
    zf = z[:, :H_FOX] + fq[:, :, :, None] - fk[:, :, None, :]
    p_fox = jax.nn.softmax(jnp.where(causal, zf, -jnp.inf), axis=-1)
    zs = z[:, H_FOX:]
    log_keep = jnp.where(strict, jax.nn.log_sigmoid(-zs), 0.0)
    later = lax.cumsum(log_keep, axis=3, reverse=True) - log_keep
    a_sb = jnp.where(strict, jnp.exp(jax.nn.log_sigmoid(zs) + later), 0.0)
    return jnp.concatenate([p_fox, a_sb], axis=1)


def _attend_prompt(q, k, v, logf):
    b, s = q.shape[:2]
    nb = s // Q_BLOCK
    fk = jnp.cumsum(logf, axis=1).transpose(0, 2, 1)
    k_pos = jnp.arange(s)
    qb = q.reshape(b, nb, Q_BLOCK, N_HEADS, HEAD_DIM).swapaxes(0, 1)

    def block(args):
        qi, start = args
        q_pos = start + jnp.arange(Q_BLOCK)
        fq = lax.dynamic_slice_in_dim(fk, start, Q_BLOCK, axis=2)
        z = jnp.einsum("bqhd,bkhd->bhqk", qi, k, preferred_element_type=jnp.float32) * ATTN_SCALE
        w = _head_weights(z, fq, fk, q_pos, k_pos).astype(v.dtype)
        return jnp.einsum("bhqk,bkhd->bqhd", w, v)

    out = lax.map(block, (qb, jnp.arange(nb) * Q_BLOCK))
    return out.swapaxes(0, 1).reshape(b, s, N_HEADS, HEAD_DIM)


def _attend_sample(q, k, v, logf, k_pages, v_pages, logf_pages):
    b, t = q.shape[:2]
    n_pages, page = k_pages.shape[1], k_pages.shape[2]
    past = n_pages * page
    logf_all = jnp.concatenate(
        [logf_pages.reshape(b, past, H_FOX).astype(jnp.float32), logf], axis=1)
    fk = jnp.cumsum(logf_all, axis=1).transpose(0, 2, 1)
    fq = fk[:, :, past:]
    k_pos = jnp.arange(past + t)
    q_pos = past + jnp.arange(t)
    z_past = jnp.einsum("bqhd,bpshd->bhqps", q, k_pages,
                        preferred_element_type=jnp.float32).reshape(b, N_HEADS, t, past)
    z_new = jnp.einsum("bqhd,bkhd->bhqk", q, k, preferred_element_type=jnp.float32)
    z = jnp.concatenate([z_past, z_new], axis=-1) * ATTN_SCALE
    w = _head_weights(z, fq, fk, q_pos, k_pos).astype(v.dtype)
    w_past = w[..., :past].reshape(b, N_HEADS, t, n_pages, page)
    return (jnp.einsum("bhqps,bpshd->bqhd", w_past, v_pages)
            + jnp.einsum("bhqk,bkhd->bqhd", w[..., past:], v))


def _moe(h, w_router, b_router, w_gate, w_up, w_down, ws_gate, ws_up, ws_down):
    shape = h.shape
    t = h.reshape(-1, shape[-1])
    n = t.shape[0]
    scores = jax.nn.sigmoid(jnp.dot(t, w_router, preferred_element_type=jnp.float32))
    sel = scores + b_router.astype(jnp.float32)
    grp = lax.top_k(sel.reshape(n, N_GROUPS, N_EXPERTS // N_GROUPS), 2)[0].sum(-1)
    _, gidx = lax.top_k(grp, TOPK_GROUPS)
    gmask = jax.nn.one_hot(gidx, N_GROUPS, dtype=jnp.float32).sum(axis=1) > 0
    emask = jnp.repeat(gmask, N_EXPERTS // N_GROUPS, axis=-1)
    _, eidx = lax.top_k(jnp.where(emask, sel, -jnp.inf), TOP_K)
    wk = jnp.take_along_axis(scores, eidx, axis=-1)
    wk = wk / jnp.sum(wk, axis=-1, keepdims=True) * ROUTED_SCALE
    gates = jnp.einsum("nk,nke->en", wk, jax.nn.one_hot(eidx, N_EXPERTS, dtype=jnp.float32))

    def expert(acc, xs):
        wg, wu, wd, g = xs
        y = jnp.dot(jax.nn.silu(t @ wg) * (t @ wu), wd, preferred_element_type=jnp.float32)
        return acc + g[:, None] * y, None

    routed, _ = lax.scan(expert, jnp.zeros((n, shape[-1]), jnp.float32),
                         (w_gate, w_up, w_down, gates))
    shared = jnp.dot(jax.nn.silu(t @ ws_gate) * (t @ ws_up), ws_down,
                     preferred_element_type=jnp.float32)
    return (routed + shared).astype(h.dtype).reshape(shape)


def _layer(x, c, attend, w_ada, b_ada, g_attn, w_in, b_f, g_out_fox, g_out_sb, w_out,
           g_ffn, w_router, b_router, w_gate, w_up, w_down, ws_gate, ws_up, ws_down):
    b, t = x.shape[:2]
    mod = (jax.nn.silu(c) @ w_ada + b_ada)[:, None, :]
    sh1, sc1, ga1, sh2, sc2, ga2 = jnp.split(mod, N_MOD, axis=-1)
    h = _rmsnorm(x, g_attn) * (1.0 + sc1) + sh1
    proj = h @ w_in
    qkv = proj[..., :3 * MIX_WIDTH].reshape(b, t, 3, N_HEADS, HEAD_DIM)
    q, k, v = qkv[:, :, 0], qkv[:, :, 1], qkv[:, :, 2]
    logf = jax.nn.log_sigmoid(proj[..., 3 * MIX_WIDTH:].astype(jnp.float32)
                              + b_f.astype(jnp.float32))
    o = attend(q, k, v, logf)
    o_fox = _rmsnorm(o[:, :, :H_FOX].reshape(b, t, H_FOX * HEAD_DIM), g_out_fox)
    o_sb = _rmsnorm(o[:, :, H_FOX:].reshape(b, t, H_SB * HEAD_DIM), g_out_sb)
    x = x + ga1 * (jnp.concatenate([o_fox, o_sb], axis=-1) @ w_out)
    h = _rmsnorm(x, g_ffn) * (1.0 + sc2) + sh2
    x = x + ga2 * _moe(h, w_router, b_router, w_gate, w_up, w_down, ws_gate, ws_up, ws_down)
    return x, k, v, logf


def setup_inputs(seed: int = 0) -> dict:
    key = jax.random.key(seed)
    ks = iter(jax.random.split(key, 32))
    f32 = jnp.float32

    def nrm(shape, scale):
        return jax.random.normal(next(ks), shape, f32) * scale

    def gain(shape):
        return 1.0 + nrm(shape, 0.05)

    n_pages = PAST_LEN // PAGE_SIZE
    n_used = DEC_BATCH * n_pages
    n_phys = (5 * n_used + 3) // 4
    d = D_MODEL
    inputs = {}
    inputs["x_prompt"] = jax.random.normal(next(ks), (BATCH, SEQ, d), f32)
    inputs["x_sample"] = jax.random.normal(next(ks), (DEC_BATCH, DEC_SEQ, d), f32)
    inputs["c_prompt"] = jax.random.normal(next(ks), (BATCH, d), f32)
    inputs["c_sample"] = jax.random.normal(next(ks), (DEC_BATCH, d), f32)
    inputs["cache_k"] = jax.random.normal(next(ks), (DEPTH, n_phys, PAGE_SIZE, N_HEADS, HEAD_DIM), f32)
    inputs["cache_v"] = jax.random.normal(next(ks), (DEPTH, n_phys, PAGE_SIZE, N_HEADS, HEAD_DIM), f32)
    inputs["cache_logf"] = jax.nn.log_sigmoid(
        3.0 + jax.random.normal(next(ks), (DEPTH, n_phys, PAGE_SIZE, H_FOX), f32))
    inputs["page_table"] = jax.random.permutation(next(ks), n_phys)[:n_used].reshape(
        DEC_BATCH, n_pages).astype(jnp.int32)
    inputs["w_ada"] = nrm((DEPTH, d, N_MOD * d), 0.5 * d ** -0.5)
    inputs["b_ada"] = nrm((DEPTH, N_MOD * d), 0.02)
    inputs["g_attn"] = gain((DEPTH, d))
    inputs["w_in"] = nrm((DEPTH, d, 3 * MIX_WIDTH + H_FOX), d ** -0.5)
    inputs["b_f"] = 3.0 + nrm((DEPTH, H_FOX), 0.5)
    inputs["g_out_fox"] = gain((DEPTH, H_FOX * HEAD_DIM))
    inputs["g_out_sb"] = gain((DEPTH, H_SB * HEAD_DIM))
    inputs["w_out"] = nrm((DEPTH, MIX_WIDTH, d), MIX_WIDTH ** -0.5)
    inputs["g_ffn"] = gain((DEPTH, d))
    inputs["w_router"] = nrm((DEPTH, d, N_EXPERTS), d ** -0.5)
    inputs["b_router"] = nrm((DEPTH, N_EXPERTS), 0.01)
    inputs["w_gate"] = nrm((DEPTH, N_EXPERTS, d, EXPERT_FF), d ** -0.5)
    inputs["w_up"] = nrm((DEPTH, N_EXPERTS, d, EXPERT_FF), d ** -0.5)
    inputs["w_down"] = nrm((DEPTH, N_EXPERTS, EXPERT_FF, d), EXPERT_FF ** -0.5)
    inputs["ws_gate"] = nrm((DEPTH, d, SHARED_FF), d ** -0.5)
    inputs["ws_up"] = nrm((DEPTH, d, SHARED_FF), d ** -0.5)
    inputs["ws_down"] = nrm((DEPTH, SHARED_FF, d), SHARED_FF ** -0.5)
    inputs["g_final"] = gain((d,))
    return inputs


def reference(x_prompt, x_sample, c_prompt, c_sample, cache_k, cache_v, cache_logf, page_table,
              w_ada, b_ada, g_attn, w_in, b_f, g_out_fox, g_out_sb, w_out, g_ffn,
              w_router, b_router, w_gate, w_up, w_down, ws_gate, ws_up, ws_down, g_final):
    yp, ys = x_prompt, x_sample
    kp, vp, fp, k_s, v_s, f_s = [], [], [], [], [], []
    for l in range(DEPTH):
        lw = (w_ada[l], b_ada[l], g_attn[l], w_in[l], b_f[l], g_out_fox[l], g_out_sb[l],
              w_out[l], g_ffn[l], w_router[l], b_router[l], w_gate[l], w_up[l], w_down[l],
              ws_gate[l], ws_up[l], ws_down[l])
        yp, k1, v1, f1 = _layer(yp, c_prompt, _attend_prompt, *lw)
        attend_s = functools.partial(_attend_sample,
                                     k_pages=cache_k[l, page_table],
                                     v_pages=cache_v[l, page_table],
                                     logf_pages=cache_logf[l, page_table])
        ys, k2, v2, f2 = _layer(ys, c_sample, attend_s, *lw)
        kp.append(k1); vp.append(v1); fp.append(f1)
        k_s.append(k2); v_s.append(v2); f_s.append(f2)
    y_prompt = _rmsnorm(yp, g_final)
    y_sample = _rmsnorm(ys, g_final)
    k_prompt = jnp.stack(kp)
    v_prompt = jnp.stack(vp)
    logf_prompt = jnp.stack(fp).astype(cache_logf.dtype)
    k_sample = jnp.stack(k_s)
    v_sample = jnp.stack(v_s)
    logf_sample = jnp.stack(f_s).astype(cache_logf.dtype)
    return (y_prompt, y_sample, k_prompt, v_prompt, logf_prompt, k_sample, v_sample, logf_sample)
```

```python
import functools

import jax
import jax.numpy as jnp
from jax import lax
from jax.experimental import pallas as pl
from jax.experimental.pallas import tpu as pltpu

F32 = jnp.float32
BF16 = jnp.bfloat16

HEAD_DIM = 64
N_GROUPS = 8
TOPK_GROUPS = 4
TOP_K = 6
ROUTED_SCALE = 2.5
N_MOD = 6
EPS = 1e-6
LANES = 128
NEG_INF = float("-inf")

VMEM_LIMIT = 56 * 1024 * 1024


def _cparams(*sem):
    return pltpu.CompilerParams(dimension_semantics=sem, vmem_limit_bytes=VMEM_LIMIT)


def _split_bf16(x, n):
    parts = []
    r = x
    for _ in range(n):
        p = r.astype(BF16)
        parts.append(p)
        r = r - p.astype(F32)
    return parts


def _dot(a, b):
    return jnp.dot(a, b, preferred_element_type=F32)


def _dot_nt(a, b):
    return lax.dot_general(a, b, (((1,), (1,)), ((), ())), preferred_element_type=F32)


def _dot_f32(a, b):
    ah, al = _split_bf16(a, 2)
    bh, bl = _split_bf16(b, 2)
    return _dot(ah, bh) + (_dot(ah, bl) + _dot(al, bh))


def _dot_exact_rhs(a, b_exact):
    a0, a1, a2 = _split_bf16(a, 3)
    return _dot(a0, b_exact) + (_dot(a1, b_exact) + _dot(a2, b_exact))


def _sigmoid(x):
    return 1.0 / (1.0 + jnp.exp(-x))


def _softplus_parts(z):
    return jnp.maximum(z, 0.0) + jnp.log1p(jnp.exp(-jnp.abs(z)))


def _rms(x, g):
    return x * lax.rsqrt(jnp.mean(x * x, axis=-1, keepdims=True) + EPS) * g


def _mod_kernel(c_ref, w_ref, b_ref, o_ref):
    c = c_ref[...]
    s = c * _sigmoid(c)
    o_ref[...] = _dot_f32(s, w_ref[...]) + b_ref[...]


def _mod(c_pad, w_ada, b_ada):
    r, d = c_pad.shape
    n = w_ada.shape[1]
    return pl.pallas_call(
        _mod_kernel,
        grid=(n // d,),
        in_specs=[pl.BlockSpec((r, d), lambda j: (0, 0)),
                  pl.BlockSpec((d, d), lambda j: (0, j)),
                  pl.BlockSpec((1, d), lambda j: (0, j))],
        out_specs=pl.BlockSpec((r, d), lambda j: (0, j)),
        out_shape=jax.ShapeDtypeStruct((r, n), F32),
        compiler_params=_cparams("arbitrary"),
        name="mod",
    )(c_pad, w_ada, b_ada.reshape(1, n))


def _inproj_kernel(x_ref, sh_ref, sc_ref, g_ref, wq_ref, wk_ref, wv_ref, wf_ref, bf_ref, tri_ref,
                   q_ref, kb_ref, vb_ref, k_ref, v_ref, lf_ref, cum_ref, carry_ref):
    @pl.when(pl.program_id(0) == 0)
    def _():
        carry_ref[...] = jnp.zeros_like(carry_ref)

    h = _rms(x_ref[...], g_ref[...]) * (1.0 + sc_ref[...]) + sh_ref[...]
    hb = h.astype(BF16)
    q = _dot(hb, wq_ref[...])
    q_ref[...] = (q * (HEAD_DIM ** -0.5)).astype(BF16)
    k = _dot(hb, wk_ref[...])
    k_ref[...] = k
    kb_ref[...] = k.astype(BF16)
    v = _dot(hb, wv_ref[...])
    v_ref[...] = v
    vb_ref[...] = v.astype(BF16)
    fl = _dot_f32(h, wf_ref[...]) + bf_ref[...]
    lf = jnp.minimum(fl, 0.0) - jnp.log1p(jnp.exp(-jnp.abs(fl)))
    lf_ref[...] = lf
    l0, l1, l2 = _split_bf16(lf, 3)
    tri = tri_ref[...]
    cum = _dot(tri, l0) + (_dot(tri, l1) + _dot(tri, l2)) + carry_ref[...]
    cum_ref[...] = cum
    carry_ref[...] = cum[-1:, :]


def _inproj(x, mod, g_attn, wq, wk, wv, wf_pad, bf_pad, tm):
    t, d = x.shape
    per_token = mod.shape[0] != 1
    mrow = (lambda i: i) if per_token else (lambda i: 0)
    mr = tm if per_token else 1
    tri = (jnp.arange(tm)[None, :] <= jnp.arange(tm)[:, None]).astype(BF16)
    full = lambda shape: pl.BlockSpec(shape, lambda i: (0, 0))
    tok = lambda: pl.BlockSpec((tm, d), lambda i: (i, 0))
    return pl.pallas_call(
        _inproj_kernel,
        grid=(t // tm,),
        in_specs=[tok(),
                  pl.BlockSpec((mr, d), lambda i: (mrow(i), 0)),
                  pl.BlockSpec((mr, d), lambda i: (mrow(i), 1)),
                  full((1, d)), full((d, d)), full((d, d)), full((d, d)),
                  full((d, LANES)), full((1, LANES)), full((tm, tm))],
        out_specs=[tok(), tok(), tok(), tok(), tok(),
                   pl.BlockSpec((tm, LANES), lambda i: (i, 0)),
                   pl.BlockSpec((tm, LANES), lambda i: (i, 0))],
        out_shape=[jax.ShapeDtypeStruct((t, d), BF16)] * 3
                  + [jax.ShapeDtypeStruct((t, d), F32)] * 2
                  + [jax.ShapeDtypeStruct((t, LANES), F32)] * 2,
        scratch_shapes=[pltpu.VMEM((1, LANES), F32)],
        compiler_params=_cparams("arbitrary"),
        name="inproj",
    )(x, mod, mod, g_attn, wq, wk, wv, wf_pad, bf_pad, tri)


def _head_masks(q):
    lane = lax.broadcasted_iota(jnp.int32, q.shape, 1)
    zero = jnp.zeros_like(q)
    return jnp.where(lane < HEAD_DIM, q, zero), jnp.where(lane >= HEAD_DIM, q, zero)


def _fox_kernel(q_ref, k_ref, v_ref, cum_ref, o_ref, *, tq):
    i = pl.program_id(1)
    q0 = i * tq
    row = lax.broadcasted_iota(jnp.int32, (tq, tq), 0)
    col = lax.broadcasted_iota(jnp.int32, (tq, tq), 1)
    causal = col <= row
    lane = lax.broadcasted_iota(jnp.int32, (tq, 2 * HEAD_DIM), 1)
    outs = []
    for hh, qh in enumerate(_head_masks(q_ref[...])):
        c_end = cum_ref[hh:hh + 1, pl.ds(q0 + tq - LANES, LANES)][:, LANES - 1:]

        def scores(k0, qh=qh, hh=hh, c_end=c_end):
            z = _dot_nt(qh, k_ref[pl.ds(k0, tq), :])
            return z + (c_end - cum_ref[hh:hh + 1, pl.ds(k0, tq)])

        s = jnp.where(causal, scores(q0), NEG_INF)
        m = jnp.max(s, axis=1, keepdims=True)
        p = jnp.exp(s - m)
        l = jnp.sum(p, axis=1, keepdims=True)
        acc = _dot(p.astype(BF16), v_ref[pl.ds(q0, tq), :])

        def body(j, st, scores=scores):
            m, l, acc = st
            k0 = pl.multiple_of(j * tq, tq)
            s = scores(k0)
            m_new = jnp.maximum(m, jnp.max(s, axis=1, keepdims=True))
            alpha = jnp.exp(m - m_new)
            p = jnp.exp(s - m_new)
            l = alpha * l + jnp.sum(p, axis=1, keepdims=True)
            acc = alpha * acc + _dot(p.astype(BF16), v_ref[pl.ds(k0, tq), :])
            return m_new, l, acc

        m, l, acc = lax.fori_loop(0, i, body, (m, l, acc))
        outs.append(acc / l)
    o_ref[...] = jnp.where(lane < HEAD_DIM, outs[0], outs[1])


def _sb_kernel(q_ref, k_ref, v_ref, tri_ref, o_ref, *, tq):
    i = pl.program_id(1)
    q0 = i * tq
    row = lax.broadcasted_iota(jnp.int32, (tq, tq), 0)
    col = lax.broadcasted_iota(jnp.int32, (tq, tq), 1)
    strict = col < row
    lane = lax.broadcasted_iota(jnp.int32, (tq, 2 * HEAD_DIM), 1)
    outs = []
    for qh in _head_masks(q_ref[...]):

        def block(k0, carry, acc, mask, qh=qh):
            z = _dot_nt(qh, k_ref[pl.ds(k0, tq), :])
            keep = -_softplus_parts(z)
            lk = keep if mask is None else jnp.where(mask, keep, 0.0)
            hi, lo = _split_bf16(lk, 2)
            tri = tri_ref[...]
            later = _dot(hi, tri) + _dot(lo, tri) + carry
            a = jnp.exp(z + keep + later)
            if mask is not None:
                a = jnp.where(mask, a, 0.0)
            acc = acc + _dot(a.astype(BF16), v_ref[pl.ds(k0, tq), :])
            return carry + jnp.sum(lk, axis=1, keepdims=True), acc

        carry, acc = block(q0, jnp.zeros((tq, 1), F32),
                           jnp.zeros((tq, 2 * HEAD_DIM), F32), strict)

        def body(j, st, block=block):
            k0 = pl.multiple_of((i - 1 - j) * tq, tq)
            return block(k0, st[0], st[1], None)

        carry, acc = lax.fori_loop(0, i, body, (carry, acc))
        outs.append(acc)
    o_ref[...] = jnp.where(lane < HEAD_DIM, outs[0], outs[1])


def _prompt_attention(qb, kb, vb, cum_pairs, tq):
    s, d = qb.shape
    n_pairs = d // (2 * HEAD_DIM)
    fox_pairs = cum_pairs.shape[0]
    sb_pairs = n_pairs - fox_pairs
    w = 2 * HEAD_DIM
    nq = s // tq
    o_fox = pl.pallas_call(
        functools.partial(_fox_kernel, tq=tq),
        grid=(fox_pairs, nq),
        in_specs=[pl.BlockSpec((tq, w), lambda p, i: (i, p)),
                  pl.BlockSpec((s, w), lambda p, i: (0, p)),
                  pl.BlockSpec((s, w), lambda p, i: (0, p)),
                  pl.BlockSpec((None, 2, s), lambda p, i: (p, 0, 0))],
        out_specs=pl.BlockSpec((tq, w), lambda p, i: (i, p)),
        out_shape=jax.ShapeDtypeStruct((s, fox_pairs * w), F32),
        compiler_params=_cparams("arbitrary", "arbitrary"),
        name="fox_attention",
    )(qb, kb, vb, cum_pairs)
    tri = (jnp.arange(tq)[:, None] > jnp.arange(tq)[None, :]).astype(BF16)
    o_sb = pl.pallas_call(
        functools.partial(_sb_kernel, tq=tq),
        grid=(sb_pairs, nq),
        in_specs=[pl.BlockSpec((tq, w), lambda p, i: (i, p + fox_pairs)),
                  pl.BlockSpec((s, w), lambda p, i: (0, p + fox_pairs)),
                  pl.BlockSpec((s, w), lambda p, i: (0, p + fox_pairs)),
                  pl.BlockSpec((tq, tq), lambda p, i: (0, 0))],
        out_specs=pl.BlockSpec((tq, w), lambda p, i: (i, p)),
        out_shape=jax.ShapeDtypeStruct((s, sb_pairs * w), F32),
        compiler_params=_cparams("arbitrary", "arbitrary"),
        name="sb_attention",
    )(qb, kb, vb, tri)
    return o_fox, o_sb


def _sample_kernel(pt_ref, q_ref, kn_ref, vn_ref, lfn_ref, kc_ref, vc_ref, lfc_ref, tri_ref, o_ref,
                   m_ref, l_ref, carry_ref, acc_ref, *, n_tok, h_fox):
    del pt_ref
    s = pl.program_id(1)
    n_steps = pl.num_programs(1)
    rows, d = acc_ref.shape
    half = rows // 2

    def process(z, lf, fox_mask, sb_mask, pv):
        zf, zs = z[:half], z[half:]
        lfe = jnp.concatenate([lf] * n_tok, axis=0)
        keep = -_softplus_parts(zs)
        lk = keep if sb_mask is None else jnp.where(sb_mask, keep, 0.0)
        dec = jnp.concatenate([lfe, lk], axis=0)
        hi, lo = _split_bf16(dec, 2)
        tri = tri_ref[...]
        later = _dot(hi, tri) + _dot(lo, tri) + carry_ref[...]
        carry_ref[...] += jnp.sum(dec, axis=1, keepdims=True)
        logit = zf + later[:half]
        if fox_mask is not None:
            logit = jnp.where(fox_mask, logit, NEG_INF)
        m_old = m_ref[...]
        m_new = jnp.maximum(m_old, jnp.max(logit, axis=1, keepdims=True))
        alpha = jnp.exp(m_old - m_new)
        p = jnp.exp(logit - m_new)
        l_ref[...] = alpha * l_ref[...] + jnp.sum(p, axis=1, keepdims=True)
        m_ref[...] = m_new
        a = jnp.exp(zs + keep + later[half:])
        if sb_mask is not None:
            a = jnp.where(sb_mask, a, 0.0)
        w = jnp.concatenate([p, a], axis=0).astype(BF16)
        scale = jnp.concatenate([alpha, jnp.ones_like(alpha)], axis=0)
        acc_ref[...] = scale * acc_ref[...] + pv(w)

    @pl.when(s == 0)
    def _():
        m_ref[...] = jnp.full_like(m_ref, NEG_INF)
        l_ref[...] = jnp.zeros_like(l_ref)
        carry_ref[...] = jnp.zeros_like(carry_ref)
        acc_ref[...] = jnp.zeros_like(acc_ref)
        pad = jnp.zeros((LANES - kn_ref.shape[0], d), BF16)
        kn = jnp.concatenate([kn_ref[...], pad], axis=0)
        vn = jnp.concatenate([vn_ref[...], pad], axis=0)
        z = _dot_nt(q_ref[...], kn)
        key = lax.broadcasted_iota(jnp.int32, (half, LANES), 1)
        tok = (lax.broadcasted_iota(jnp.int32, (half, LANES), 0) // h_fox) % n_tok
        process(z, lfn_ref[...], key <= tok, key < tok, lambda w: _dot(w, vn))

    @pl.when(s > 0)
    def _():
        z = _dot(q_ref[...], kc_ref[...].astype(BF16))
        vt = vc_ref[...].astype(BF16)
        process(z, lfc_ref[...], None, None, lambda w: _dot_nt(w, vt))

    @pl.when(s == n_steps - 1)
    def _():
        l = jnp.concatenate([l_ref[...], jnp.ones((half, 1), F32)], axis=0)
        acc = acc_ref[...] / l
        r = lax.broadcasted_iota(jnp.int32, (rows, d), 0)
        c = lax.broadcasted_iota(jnp.int32, (rows, d), 1)
        head_of_row = (r // half) * h_fox + r % h_fox
        own = jnp.where(head_of_row == c // HEAD_DIM, acc, 0.0)
        t = jnp.sum(own.reshape(rows // 8, 8, d), axis=1)
        o_ref[...] = t + pltpu.roll(t, shift=n_tok, axis=0)


def _sample_attention(page_table, q_blk, kn, vn, lfn, cache_kt, cache_vt, cache_lft, n_tok, h_fox):
    b, rows, d = q_blk.shape
    n_pages = page_table.shape[1]
    page = cache_kt.shape[2]
    assert page == LANES and rows == 2 * n_tok * h_fox and 2 * n_tok == 8
    tri = (jnp.arange(page)[:, None] > jnp.arange(page)[None, :]).astype(BF16)

    def page_idx(i, s, pt):
        return pt[i * n_pages + (n_pages - jnp.maximum(s, 1))]

    seq = lambda shape: pl.BlockSpec((None,) + shape, lambda i, s, pt: (i, 0, 0))
    cache = lambda shape: pl.BlockSpec((None,) + shape, lambda i, s, pt: (page_idx(i, s, pt), 0, 0))
    grid_spec = pltpu.PrefetchScalarGridSpec(
        num_scalar_prefetch=1,
        grid=(b, n_pages + 1),
        in_specs=[seq((rows, d)), seq(kn.shape[1:]), seq(vn.shape[1:]), seq(lfn.shape[1:]),
                  cache((d, page)), cache((d, page)), cache((h_fox, page)),
                  pl.BlockSpec((page, page), lambda i, s, pt: (0, 0))],
        out_specs=seq((8, d)),
        scratch_shapes=[pltpu.VMEM((rows // 2, 1), F32), pltpu.VMEM((rows // 2, 1), F32),
                        pltpu.VMEM((rows, 1), F32), pltpu.VMEM((rows, d), F32)],
    )
    return pl.pallas_call(
        functools.partial(_sample_kernel, n_tok=n_tok, h_fox=h_fox),
        grid_spec=grid_spec,
        out_shape=jax.ShapeDtypeStruct((b, 8, d), F32),
        compiler_params=_cparams("arbitrary", "arbitrary"),
        name="sample_attention",
    )(page_table.reshape(-1), q_blk, kn, vn, lfn, cache_kt, cache_vt, cache_lft, tri)


def _route(sel, scores, n_experts):
    lane = lax.broadcasted_iota(jnp.int32, sel.shape, 1)
    valid = lane < n_experts
    per_group = n_experts // N_GROUPS
    group = lane // per_group
    big = jnp.int32(1 << 20)

    def first_max(v, ids):
        mx = jnp.max(v, axis=1, keepdims=True)
        return mx, jnp.min(jnp.where(v == mx, ids, big), axis=1, keepdims=True)

    gscore = jnp.full(sel.shape, NEG_INF, F32)
    for g in range(N_GROUPS):
        in_g = group == g
        v = jnp.where(in_g, sel, NEG_INF)
        m1, i1 = first_max(v, lane)
        m2, _ = first_max(jnp.where(lane == i1, NEG_INF, v), lane)
        gscore = jnp.where(in_g, m1 + m2, gscore)
    allowed = jnp.zeros(sel.shape, jnp.bool_)
    for _ in range(TOPK_GROUPS):
        _, gi = first_max(gscore, group)
        pick = group == gi
        allowed = jnp.logical_or(allowed, pick)
        gscore = jnp.where(pick, NEG_INF, gscore)
    v = jnp.where(jnp.logical_and(allowed, valid), sel, NEG_INF)
    raw = jnp.zeros(sel.shape, F32)
    for _ in range(TOP_K):
        _, ei = first_max(v, lane)
        pick = lane == ei
        raw = jnp.where(pick, scores, raw)
        v = jnp.where(pick, NEG_INF, v)
    gates = raw / jnp.sum(raw, axis=1, keepdims=True) * ROUTED_SCALE
    return jnp.where(lane == n_experts, 1.0, gates)


def _post_kernel(of_ref, os_ref, x_ref, ga_ref, sh_ref, sc_ref, gf_ref, gs_ref, wo_ref, gn_ref,
                 wr_ref, br_ref, x1_ref, h2_ref, gate_ref, *, n_experts):
    o = jnp.concatenate([_rms(of_ref[...], gf_ref[...]), _rms(os_ref[...], gs_ref[...])], axis=1)
    x1 = x_ref[...] + ga_ref[...] * _dot(o.astype(BF16), wo_ref[...])
    x1_ref[...] = x1
    h2 = _rms(x1, gn_ref[...]) * (1.0 + sc_ref[...]) + sh_ref[...]
    h2_ref[...] = h2.astype(BF16)
    scores = _sigmoid(_dot_f32(h2, wr_ref[...]))
    gate_ref[...] = _route(scores + br_ref[...], scores, n_experts)


def _post(o_fox, o_sb, x, mod, g_fox, g_sb, w_out, g_ffn, wr_pad, br_pad, n_experts, tm):
    t, d = x.shape
    per_token = mod.shape[0] != 1
    mrow = (lambda i: i) if per_token else (lambda i: 0)
    mr = tm if per_token else 1
    full = lambda shape: pl.BlockSpec(shape, lambda i: (0, 0))
    tok = lambda w: pl.BlockSpec((tm, w), lambda i: (i, 0))
    modspec = lambda j: pl.BlockSpec((mr, d), lambda i: (mrow(i), j))
    return pl.pallas_call(
        functools.partial(_post_kernel, n_experts=n_experts),
        grid=(t // tm,),
        in_specs=[tok(d // 2), tok(d // 2), tok(d), modspec(2), modspec(3), modspec(4),
                  full((1, d // 2)), full((1, d // 2)), full((d, d)), full((1, d)),
                  full((d, LANES)), full((1, LANES))],
        out_specs=[tok(d), tok(d), tok(LANES)],
        out_shape=[jax.ShapeDtypeStruct((t, d), F32), jax.ShapeDtypeStruct((t, d), BF16),
                   jax.ShapeDtypeStruct((t, LANES), F32)],
        compiler_params=_cparams("arbitrary"),
        name="post_attention",
    )(o_fox, o_sb, x, mod, mod, mod, g_fox, g_sb, w_out, g_ffn, wr_pad, br_pad)


def _moe_kernel(h_ref, gate_ref, wgu_ref, wd_ref, x1_ref, ga_ref, gfin_ref, y_ref, acc_ref):
    e = pl.program_id(1)

    @pl.when(e == 0)
    def _():
        acc_ref[...] = jnp.zeros_like(acc_ref)

    gu = _dot(h_ref[...], wgu_ref[...])
    ff = gu.shape[1] // 2
    gpre = gu[:, :ff]
    act = gpre * _sigmoid(gpre) * gu[:, ff:]
    gates = gate_ref[...]
    lane = lax.broadcasted_iota(jnp.int32, gates.shape, 1)
    g = jnp.sum(jnp.where(lane == e, gates, 0.0), axis=1, keepdims=True)
    acc_ref[...] += _dot((act * g).astype(BF16), wd_ref[...])

    @pl.when(e == pl.num_programs(1) - 1)
    def _():
        x2 = x1_ref[...] + ga_ref[...] * acc_ref[...]
        y_ref[...] = _rms(x2, gfin_ref[...])


def _moe(h2, gates, wgu, wd, x1, mod, g_final, tm):
    t, d = x1.shape
    n_e, _, ff2 = wgu.shape
    per_token = mod.shape[0] != 1
    mrow = (lambda i: i) if per_token else (lambda i: 0)
    mr = tm if per_token else 1
    return pl.pallas_call(
        _moe_kernel,
        grid=(t // tm, n_e),
        in_specs=[pl.BlockSpec((tm, d), lambda i, e: (i, 0)),
                  pl.BlockSpec((tm, LANES), lambda i, e: (i, 0)),
                  pl.BlockSpec((None, d, ff2), lambda i, e: (e, 0, 0)),
                  pl.BlockSpec((None, ff2 // 2, d), lambda i, e: (e, 0, 0)),
                  pl.BlockSpec((tm, d), lambda i, e: (i, 0)),
                  pl.BlockSpec((mr, d), lambda i, e: (mrow(i), 5)),
                  pl.BlockSpec((1, d), lambda i, e: (0, 0))],
        out_specs=pl.BlockSpec((tm, d), lambda i, e: (i, 0)),
        out_shape=jax.ShapeDtypeStruct((t, d), F32),
        scratch_shapes=[pltpu.VMEM((tm, d), F32)],
        compiler_params=_cparams("arbitrary", "arbitrary"),
        name="experts",
    )(h2, gates, wgu, wd, x1, mod, g_final)


def _pad_lanes(a):
    return jnp.pad(a, ((0, 0), (0, LANES - a.shape[1])))


def _tile(n, pref):
    return pref if n % pref == 0 else n


def kernel(x_prompt, x_sample, c_prompt, c_sample, cache_k, cache_v, cache_logf, page_table, w_ada, b_ada, g_attn, w_in, b_f, g_out_fox, g_out_sb, w_out, g_ffn, w_router, b_router, w_gate, w_up, w_down, ws_gate, ws_up, ws_down, g_final):
    depth = w_ada.shape[0]
    assert depth == 1 and x_prompt.shape[0] == 1
    _, seq, d = x_prompt.shape
    dec_b, n_tok, _ = x_sample.shape
    n_heads = d // HEAD_DIM
    h_fox = w_in.shape[2] - 3 * d
    n_experts = w_router.shape[2]
    page = cache_k.shape[2]

    wq = w_in[0, :, :d].astype(BF16)
    wk = w_in[0, :, d:2 * d].astype(BF16)
    wv = w_in[0, :, 2 * d:3 * d].astype(BF16)
    wf_pad = _pad_lanes(w_in[0, :, 3 * d:])
    bf_pad = _pad_lanes(b_f)
    wr_pad = _pad_lanes(w_router[0])
    br_pad = _pad_lanes(b_router)
    wgu = jnp.concatenate([jnp.concatenate([w_gate[0], w_up[0]], axis=2),
                           jnp.concatenate([ws_gate, ws_up], axis=2)], axis=0).astype(BF16)
    wd = jnp.concatenate([w_down[0], ws_down], axis=0).astype(BF16)
    wo = w_out[0].astype(BF16)

    c_all = jnp.concatenate([c_prompt, c_sample], axis=0)
    n_c = c_all.shape[0]
    c_pad = jnp.pad(c_all, ((0, -n_c % 8), (0, 0)))
    mod = _mod(c_pad, w_ada[0], b_ada[0])
    mod_p = mod[:1]
    mod_s = jnp.repeat(mod[1:1 + dec_b], n_tok, axis=0)

    xp = x_prompt.reshape(seq, d)
    xs = x_sample.reshape(dec_b * n_tok, d)
    tm_p = _tile(seq, 512)
    qb, kb, vb, k_p, v_p, lf_p, cum_p = _inproj(xp, mod_p, g_attn, wq, wk, wv, wf_pad, bf_pad, tm_p)
    qs, ks_b, vs_b, k_s, v_s, lf_s, _ = _inproj(xs, mod_s, g_attn, wq, wk, wv, wf_pad, bf_pad,
                                                dec_b * n_tok)

    cum_pairs = cum_p[:, :h_fox].T.reshape(h_fox // 2, 2, seq)
    o_fox_p, o_sb_p = _prompt_attention(qb, kb, vb, cum_pairs, _tile(seq, 256))

    cache_kt = jnp.transpose(cache_k[0], (0, 2, 3, 1)).reshape(-1, d, page)
    cache_vt = jnp.transpose(cache_v[0], (0, 2, 3, 1)).reshape(-1, d, page)
    cache_lft = jnp.transpose(cache_logf[0], (0, 2, 1))
    q3 = qs.reshape(dec_b, n_tok, d)
    r = jnp.arange(2 * n_tok * h_fox)
    row_tok = (r // h_fox) % n_tok
    row_head = (r // (n_tok * h_fox)) * h_fox + r % h_fox
    own = row_head[:, None] == (jnp.arange(d) // HEAD_DIM)[None, :]
    q_blk = jnp.where(own[None], q3[:, row_tok, :], jnp.zeros((), BF16))
    pad_rows = ((0, 0), (0, 16 - n_tok), (0, 0))
    kn = jnp.pad(ks_b.reshape(dec_b, n_tok, d), pad_rows)
    vn = jnp.pad(vs_b.reshape(dec_b, n_tok, d), pad_rows)
    lfn = jnp.transpose(lf_s[:, :h_fox].reshape(dec_b, n_tok, h_fox), (0, 2, 1))
    lfn = jnp.pad(lfn, ((0, 0), (0, 0), (0, LANES - n_tok)))
    o_s = _sample_attention(page_table, q_blk, kn, vn, lfn, cache_kt, cache_vt, cache_lft,
                            n_tok, h_fox)[:, :n_tok].reshape(dec_b * n_tok, d)

    half = d // 2
    x1_p, h2_p, gates_p = _post(o_fox_p, o_sb_p, xp, mod_p, g_out_fox, g_out_sb, wo, g_ffn,
                                wr_pad, br_pad, n_experts, tm_p)
    x1_s, h2_s, gates_s = _post(o_s[:, :half], o_s[:, half:], xs, mod_s, g_out_fox, g_out_sb, wo,
                                g_ffn, wr_pad, br_pad, n_experts, dec_b * n_tok)
    g_fin = g_final.reshape(1, d)
    y_p = _moe(h2_p, gates_p, wgu, wd, x1_p, mod_p, g_fin, _tile(seq, 1024))
    y_s = _moe(h2_s, gates_s, wgu, wd, x1_s, mod_s, g_fin, dec_b * n_tok)

    heads = lambda a, lead: a.reshape(lead + (n_heads, HEAD_DIM))
    return (y_p.reshape(1, seq, d),
            y_s.reshape(dec_b, n_tok, d),
            heads(k_p, (1, 1, seq)), heads(v_p, (1, 1, seq)),
            lf_p[:, :h_fox].reshape(1, 1, seq, h_fox).astype(cache_logf.dtype),
            heads(k_s, (1, dec_b, n_tok)), heads(v_s, (1, dec_b, n_tok)),
            lf_s[:, :h_fox].reshape(1, dec_b, n_tok, h_fox).astype(cache_logf.dtype))
```

```python
import functools

import jax
import jax.numpy as jnp
from jax import lax
from jax.experimental import pallas as pl
from jax.experimental.pallas import tpu as pltpu

F32 = jnp.float32
BF16 = jnp.bfloat16

HEAD_DIM = 64
N_GROUPS = 8
TOPK_GROUPS = 4
TOP_K = 6
ROUTED_SCALE = 2.5
N_MOD = 6
EPS = 1e-6
LANES = 128
NEG_INF = float("-inf")
EXP_UNDERFLOW = 106.0
BOUND_SLACK = 1.01

VMEM_LIMIT = 56 * 1024 * 1024


def _cparams(*sem):
    return pltpu.CompilerParams(dimension_semantics=sem, vmem_limit_bytes=VMEM_LIMIT)


def _split_bf16(x, n):
    parts = []
    r = x
    for _ in range(n):
        p = r.astype(BF16)
        parts.append(p)
        r = r - p.astype(F32)
    return parts


def _dot(a, b):
    return jnp.dot(a, b, preferred_element_type=F32)


def _dot_nt(a, b):
    return lax.dot_general(a, b, (((1,), (1,)), ((), ())), preferred_element_type=F32)


def _dot_f32(a, b):
    ah, al = _split_bf16(a, 2)
    bh, bl = _split_bf16(b, 2)
    return _dot(ah, bh) + (_dot(ah, bl) + _dot(al, bh))


def _dot_exact_rhs(a, b_exact):
    a0, a1, a2 = _split_bf16(a, 3)
    return _dot(a0, b_exact) + (_dot(a1, b_exact) + _dot(a2, b_exact))


def _sigmoid(x):
    return 1.0 / (1.0 + jnp.exp(-x))


def _softplus_parts(z):
    return jnp.maximum(z, 0.0) + jnp.log1p(jnp.exp(-jnp.abs(z)))


def _rms(x, g):
    return x * lax.rsqrt(jnp.mean(x * x, axis=-1, keepdims=True) + EPS) * g


def _mod_kernel(c_ref, w_ref, b_ref, o_ref):
    c = c_ref[...]
    s = c * _sigmoid(c)
    o_ref[...] = _dot_f32(s, w_ref[...]) + b_ref[...]


def _mod(c_pad, w_ada, b_ada):
    r, d = c_pad.shape
    n = w_ada.shape[1]
    return pl.pallas_call(
        _mod_kernel,
        grid=(n // d,),
        in_specs=[pl.BlockSpec((r, d), lambda j: (0, 0)),
                  pl.BlockSpec((d, d), lambda j: (0, j)),
                  pl.BlockSpec((1, d), lambda j: (0, j))],
        out_specs=pl.BlockSpec((r, d), lambda j: (0, j)),
        out_shape=jax.ShapeDtypeStruct((r, n), F32),
        compiler_params=_cparams("arbitrary"),
        name="mod",
    )(c_pad, w_ada, b_ada.reshape(1, n))


def _inproj_kernel(x_ref, sh_ref, sc_ref, g_ref, wq_ref, wk_ref, wv_ref, wf_ref, bf_ref, tri_ref,
                   q_ref, kb_ref, vb_ref, k_ref, v_ref, lf_ref, cum_ref, carry_ref):
    @pl.when(pl.program_id(0) == 0)
    def _():
        carry_ref[...] = jnp.zeros_like(carry_ref)

    h = _rms(x_ref[...], g_ref[...]) * (1.0 + sc_ref[...]) + sh_ref[...]
    hb = h.astype(BF16)
    q = _dot(hb, wq_ref[...])
    q_ref[...] = (q * (HEAD_DIM ** -0.5)).astype(BF16)
    k = _dot(hb, wk_ref[...])
    k_ref[...] = k
    kb_ref[...] = k.astype(BF16)
    v = _dot(hb, wv_ref[...])
    v_ref[...] = v
    vb_ref[...] = v.astype(BF16)
    fl = _dot_f32(h, wf_ref[...]) + bf_ref[...]
    lf = jnp.minimum(fl, 0.0) - jnp.log1p(jnp.exp(-jnp.abs(fl)))
    lf_ref[...] = lf
    l0, l1, l2 = _split_bf16(lf, 3)
    tri = tri_ref[...]
    cum = _dot(tri, l0) + (_dot(tri, l1) + _dot(tri, l2)) + carry_ref[...]
    cum_ref[...] = cum
    carry_ref[...] = cum[-1:, :]


def _inproj(x, mod, g_attn, wq, wk, wv, wf_pad, bf_pad, tm):
    t, d = x.shape
    per_token = mod.shape[0] != 1
    mrow = (lambda i: i) if per_token else (lambda i: 0)
    mr = tm if per_token else 1
    tri = (jnp.arange(tm)[None, :] <= jnp.arange(tm)[:, None]).astype(BF16)
    full = lambda shape: pl.BlockSpec(shape, lambda i: (0, 0))
    tok = lambda: pl.BlockSpec((tm, d), lambda i: (i, 0))
    return pl.pallas_call(
        _inproj_kernel,
        grid=(t // tm,),
        in_specs=[tok(),
                  pl.BlockSpec((mr, d), lambda i: (mrow(i), 0)),
                  pl.BlockSpec((mr, d), lambda i: (mrow(i), 1)),
                  full((1, d)), full((d, d)), full((d, d)), full((d, d)),
                  full((d, LANES)), full((1, LANES)), full((tm, tm))],
        out_specs=[tok(), tok(), tok(), tok(), tok(),
                   pl.BlockSpec((tm, LANES), lambda i: (i, 0)),
                   pl.BlockSpec((tm, LANES), lambda i: (i, 0))],
        out_shape=[jax.ShapeDtypeStruct((t, d), BF16)] * 3
                  + [jax.ShapeDtypeStruct((t, d), F32)] * 2
                  + [jax.ShapeDtypeStruct((t, LANES), F32)] * 2,
        scratch_shapes=[pltpu.VMEM((1, LANES), F32)],
        compiler_params=_cparams("arbitrary"),
        name="inproj",
    )(x, mod, mod, g_attn, wq, wk, wv, wf_pad, bf_pad, tri)


def _head_masks(q):
    lane = lax.broadcasted_iota(jnp.int32, q.shape, 1)
    zero = jnp.zeros_like(q)
    return jnp.where(lane < HEAD_DIM, q, zero), jnp.where(lane >= HEAD_DIM, q, zero)


def _fox_kernel(q_ref, k_ref, v_ref, cum_ref, o_ref, kn_ref, *, tq):
    i = pl.program_id(1)
    q0 = i * tq
    row = lax.broadcasted_iota(jnp.int32, (tq, tq), 0)
    col = lax.broadcasted_iota(jnp.int32, (tq, tq), 1)
    causal = col <= row
    lane = lax.broadcasted_iota(jnp.int32, (tq, 2 * HEAD_DIM), 1)
    first_head = lane < HEAD_DIM

    @pl.when(i == 0)
    def _():
        def chunk(c, best):
            k = k_ref[pl.ds(pl.multiple_of(c * tq, tq), tq), :].astype(F32)
            sq = k * k
            n0 = jnp.sum(jnp.where(first_head, sq, 0.0), axis=1, keepdims=True)
            n1 = jnp.sum(jnp.where(first_head, 0.0, sq), axis=1, keepdims=True)
            return jnp.maximum(best, jnp.max(jnp.where(first_head, n0, n1), axis=0, keepdims=True))

        best = lax.fori_loop(0, k_ref.shape[0] // tq, chunk, jnp.zeros((1, 2 * HEAD_DIM), F32))
        kn_ref[...] = jnp.broadcast_to(jnp.sqrt(best), kn_ref.shape)

    outs = []
    for hh, qh in enumerate(_head_masks(q_ref[...])):
        c_end = cum_ref[hh:hh + 1, pl.ds(q0 + tq - LANES, LANES)][:, LANES - 1:]
        qf = qh.astype(F32)
        kn = kn_ref[0:1, hh * HEAD_DIM:hh * HEAD_DIM + 1]
        zmax = jnp.sqrt(jnp.sum(qf * qf, axis=1, keepdims=True)) * kn * BOUND_SLACK

        def scores(k0, qh=qh, hh=hh, c_end=c_end):
            z = _dot_nt(qh, k_ref[pl.ds(k0, tq), :])
            return z + (c_end - cum_ref[hh:hh + 1, pl.ds(k0, tq)])

        def live(m, k0, hh=hh, c_end=c_end, zmax=zmax):
            start = pl.multiple_of(jnp.maximum(k0 - LANES, 0), LANES)
            prev = cum_ref[hh:hh + 1, pl.ds(start, LANES)][:, LANES - 1:]
            return (jnp.max(zmax - m + (c_end - prev)) > -EXP_UNDERFLOW).astype(jnp.int32)

        s = jnp.where(causal, scores(q0), NEG_INF)
        m = jnp.max(s, axis=1, keepdims=True)
        p = jnp.exp(s - m)
        l = jnp.sum(p, axis=1, keepdims=True)
        acc = _dot(p.astype(BF16), v_ref[pl.ds(q0, tq), :])

        def cond(st):
            return jnp.logical_and(st[0] < i, st[1] > 0)

        def body(st, scores=scores, live=live):
            j, _, m, l, acc = st
            k0 = pl.multiple_of((i - 1 - j) * tq, tq)
            s = scores(k0)
            m_new = jnp.maximum(m, jnp.max(s, axis=1, keepdims=True))
            alpha = jnp.exp(m - m_new)
            p = jnp.exp(s - m_new)
            l = alpha * l + jnp.sum(p, axis=1, keepdims=True)
            acc = alpha * acc + _dot(p.astype(BF16), v_ref[pl.ds(k0, tq), :])
            return j + 1, live(m_new, k0), m_new, l, acc

        _, _, m, l, acc = lax.while_loop(cond, body, (jnp.int32(0), live(m, q0), m, l, acc))
        outs.append(acc / l)
    o_ref[...] = jnp.where(first_head, outs[0], outs[1])


def _sb_kernel(q_ref, k_ref, v_ref, tri_ref, o_ref, *, tq):
    i = pl.program_id(1)
    q0 = i * tq
    row = lax.broadcasted_iota(jnp.int32, (tq, tq), 0)
    col = lax.broadcasted_iota(jnp.int32, (tq, tq), 1)
    strict = col < row
    lane = lax.broadcasted_iota(jnp.int32, (tq, 2 * HEAD_DIM), 1)
    outs = []
    for qh in _head_masks(q_ref[...]):

        def block(k0, carry, acc, mask, qh=qh):
            z = _dot_nt(qh, k_ref[pl.ds(k0, tq), :])
            keep = -_softplus_parts(z)
            lk = keep if mask is None else jnp.where(mask, keep, 0.0)
            hi, lo = _split_bf16(lk, 2)
            tri = tri_ref[...]
            later = _dot(hi, tri) + _dot(lo, tri) + carry
            a = jnp.exp(z + keep + later)
            if mask is not None:
                a = jnp.where(mask, a, 0.0)
            acc = acc + _dot(a.astype(BF16), v_ref[pl.ds(k0, tq), :])
            return carry + jnp.sum(lk, axis=1, keepdims=True), acc

        carry, acc = block(q0, jnp.zeros((tq, 1), F32),
                           jnp.zeros((tq, 2 * HEAD_DIM), F32), strict)

        def cond(st):
            return jnp.logical_and(st[0] < i, st[1] > 0)

        def live(carry):
            return (jnp.max(carry) > -EXP_UNDERFLOW).astype(jnp.int32)

        def body(st, block=block):
            j, _, carry, acc = st
            k0 = pl.multiple_of((i - 1 - j) * tq, tq)
            carry, acc = block(k0, carry, acc, None)
            return j + 1, live(carry), carry, acc

        _, _, carry, acc = lax.while_loop(cond, body, (jnp.int32(0), live(carry), carry, acc))
        outs.append(acc)
    o_ref[...] = jnp.where(lane < HEAD_DIM, outs[0], outs[1])


def _prompt_attention(qb, kb, vb, cum_pairs, tq):
    s, d = qb.shape
    n_pairs = d // (2 * HEAD_DIM)
    fox_pairs = cum_pairs.shape[0]
    sb_pairs = n_pairs - fox_pairs
    w = 2 * HEAD_DIM
    nq = s // tq
    o_fox = pl.pallas_call(
        functools.partial(_fox_kernel, tq=tq),
        grid=(fox_pairs, nq),
        in_specs=[pl.BlockSpec((tq, w), lambda p, i: (i, p)),
                  pl.BlockSpec((s, w), lambda p, i: (0, p)),
                  pl.BlockSpec((s, w), lambda p, i: (0, p)),
                  pl.BlockSpec((None, 2, s), lambda p, i: (p, 0, 0))],
        out_specs=pl.BlockSpec((tq, w), lambda p, i: (i, p)),
        out_shape=jax.ShapeDtypeStruct((s, fox_pairs * w), F32),
        scratch_shapes=[pltpu.VMEM((8, w), F32)],
        compiler_params=_cparams("arbitrary", "arbitrary"),
        name="fox_attention",
    )(qb, kb, vb, cum_pairs)
    tri = (jnp.arange(tq)[:, None] > jnp.arange(tq)[None, :]).astype(BF16)
    o_sb = pl.pallas_call(
        functools.partial(_sb_kernel, tq=tq),
        grid=(sb_pairs, nq),
        in_specs=[pl.BlockSpec((tq, w), lambda p, i: (i, p + fox_pairs)),
                  pl.BlockSpec((s, w), lambda p, i: (0, p + fox_pairs)),
                  pl.BlockSpec((s, w), lambda p, i: (0, p + fox_pairs)),
                  pl.BlockSpec((tq, tq), lambda p, i: (0, 0))],
        out_specs=pl.BlockSpec((tq, w), lambda p, i: (i, p)),
        out_shape=jax.ShapeDtypeStruct((s, sb_pairs * w), F32),
        compiler_params=_cparams("arbitrary", "arbitrary"),
        name="sb_attention",
    )(qb, kb, vb, tri)
    return o_fox, o_sb


def _sample_kernel(pt_ref, q_ref, kn_ref, vn_ref, lfn_ref, *rest, n_tok, h_fox, pps):
    del pt_ref
    kc_refs, vc_refs, lfc_refs = rest[:pps], rest[pps:2 * pps], rest[2 * pps:3 * pps]
    tri_ref, o_ref, m_ref, l_ref, carry_ref, acc_ref = rest[3 * pps:]
    s = pl.program_id(1)
    n_steps = pl.num_programs(1)
    rows, d = acc_ref.shape
    half = rows // 2

    def process(z, lf, fox_mask, sb_mask, pv, tri):
        zf, zs = z[:half], z[half:]
        lfe = jnp.concatenate([lf] * n_tok, axis=0)
        keep = -_softplus_parts(zs)
        lk = keep if sb_mask is None else jnp.where(sb_mask, keep, 0.0)
        dec = jnp.concatenate([lfe, lk], axis=0)
        hi, lo = _split_bf16(dec, 2)
        later = _dot(hi, tri) + _dot(lo, tri) + carry_ref[...]
        carry_ref[...] += jnp.sum(dec, axis=1, keepdims=True)
        logit = zf + later[:half]
        if fox_mask is not None:
            logit = jnp.where(fox_mask, logit, NEG_INF)
        m_old = m_ref[...]
        m_new = jnp.maximum(m_old, jnp.max(logit, axis=1, keepdims=True))
        alpha = jnp.exp(m_old - m_new)
        p = jnp.exp(logit - m_new)
        l_ref[...] = alpha * l_ref[...] + jnp.sum(p, axis=1, keepdims=True)
        m_ref[...] = m_new
        a = jnp.exp(zs + keep + later[half:])
        if sb_mask is not None:
            a = jnp.where(sb_mask, a, 0.0)
        w = jnp.concatenate([p, a], axis=0).astype(BF16)
        scale = jnp.concatenate([alpha, jnp.ones_like(alpha)], axis=0)
        acc_ref[...] = scale * acc_ref[...] + pv(w)

    @pl.when(s == 0)
    def _():
        m_ref[...] = jnp.full_like(m_ref, NEG_INF)
        l_ref[...] = jnp.zeros_like(l_ref)
        carry_ref[...] = jnp.zeros_like(carry_ref)
        acc_ref[...] = jnp.zeros_like(acc_ref)
        pad = jnp.zeros((LANES - kn_ref.shape[0], d), BF16)
        kn = jnp.concatenate([kn_ref[...], pad], axis=0)
        vn = jnp.concatenate([vn_ref[...], pad], axis=0)
        z = _dot_nt(q_ref[...], kn)
        key = lax.broadcasted_iota(jnp.int32, (half, LANES), 1)
        tok = (lax.broadcasted_iota(jnp.int32, (half, LANES), 0) // h_fox) % n_tok
        process(z, lfn_ref[...], key <= tok, key < tok, lambda w: _dot(w, vn),
                tri_ref[:LANES, :LANES])

    @pl.when(s > 0)
    def _():
        q = q_ref[...]
        z = jnp.concatenate([_dot(q, kc[...].astype(BF16)) for kc in kc_refs], axis=1)
        lf = jnp.concatenate([lfc[...] for lfc in lfc_refs], axis=1)

        def pv(w):
            out = None
            for c, vc in enumerate(vc_refs):
                part = _dot_nt(w[:, c * LANES:(c + 1) * LANES], vc[...].astype(BF16))
                out = part if out is None else out + part
            return out

        process(z, lf, None, None, pv, tri_ref[...])

    @pl.when(s == n_steps - 1)
    def _():
        l = jnp.concatenate([l_ref[...], jnp.ones((half, 1), F32)], axis=0)
        acc = acc_ref[...] / l
        r = lax.broadcasted_iota(jnp.int32, (rows, d), 0)
        c = lax.broadcasted_iota(jnp.int32, (rows, d), 1)
        head_of_row = (r // half) * h_fox + r % h_fox
        own = jnp.where(head_of_row == c // HEAD_DIM, acc, 0.0)
        t = jnp.sum(own.reshape(rows // 8, 8, d), axis=1)
        o_ref[...] = t + pltpu.roll(t, shift=n_tok, axis=0)


def _sample_attention(page_table, q_blk, kn, vn, lfn, cache_kt, cache_vt, cache_lft, n_tok, h_fox):
    b, rows, d = q_blk.shape
    n_pages = page_table.shape[1]
    page = cache_kt.shape[2]
    assert page == LANES and rows == 2 * n_tok * h_fox and 2 * n_tok == 8
    pps = next(c for c in (4, 2, 1) if n_pages % c == 0)
    keys = pps * page
    tri = (jnp.arange(keys)[:, None] > jnp.arange(keys)[None, :]).astype(BF16)

    def cache(shape, c):
        def index(i, s, pt):
            return pt[i * n_pages + n_pages - jnp.maximum(s, 1) * pps + c], 0, 0
        return pl.BlockSpec((None,) + shape, index)

    seq = lambda shape: pl.BlockSpec((None,) + shape, lambda i, s, pt: (i, 0, 0))
    grid_spec = pltpu.PrefetchScalarGridSpec(
        num_scalar_prefetch=1,
        grid=(b, n_pages // pps + 1),
        in_specs=[seq((rows, d)), seq(kn.shape[1:]), seq(vn.shape[1:]), seq(lfn.shape[1:])]
                 + [cache((d, page), c) for c in range(pps)]
                 + [cache((d, page), c) for c in range(pps)]
                 + [cache((h_fox, page), c) for c in range(pps)]
                 + [pl.BlockSpec((keys, keys), lambda i, s, pt: (0, 0))],
        out_specs=seq((8, d)),
        scratch_shapes=[pltpu.VMEM((rows // 2, 1), F32), pltpu.VMEM((rows // 2, 1), F32),
                        pltpu.VMEM((rows, 1), F32), pltpu.VMEM((rows, d), F32)],
    )
    return pl.pallas_call(
        functools.partial(_sample_kernel, n_tok=n_tok, h_fox=h_fox, pps=pps),
        grid_spec=grid_spec,
        out_shape=jax.ShapeDtypeStruct((b, 8, d), F32),
        compiler_params=_cparams("arbitrary", "arbitrary"),
        name="sample_attention",
    )(page_table.reshape(-1), q_blk, kn, vn, lfn, *([cache_kt] * pps), *([cache_vt] * pps),
      *([cache_lft] * pps), tri)


def _route(sel, scores, n_experts):
    lane = lax.broadcasted_iota(jnp.int32, sel.shape, 1)
    valid = lane < n_experts
    per_group = n_experts // N_GROUPS
    group = lane // per_group
    big = jnp.int32(1 << 20)

    def first_max(v, ids):
        mx = jnp.max(v, axis=1, keepdims=True)
        return mx, jnp.min(jnp.where(v == mx, ids, big), axis=1, keepdims=True)

    gscore = jnp.full(sel.shape, NEG_INF, F32)
    for g in range(N_GROUPS):
        in_g = group == g
        v = jnp.where(in_g, sel, NEG_INF)
        m1, i1 = first_max(v, lane)
        m2, _ = first_max(jnp.where(lane == i1, NEG_INF, v), lane)
        gscore = jnp.where(in_g, m1 + m2, gscore)
    allowed = jnp.zeros(sel.shape, jnp.bool_)
    for _ in range(TOPK_GROUPS):
        _, gi = first_max(gscore, group)
        pick = group == gi
        allowed = jnp.logical_or(allowed, pick)
        gscore = jnp.where(pick, NEG_INF, gscore)
    v = jnp.where(jnp.logical_and(allowed, valid), sel, NEG_INF)
    raw = jnp.zeros(sel.shape, F32)
    for _ in range(TOP_K):
        _, ei = first_max(v, lane)
        pick = lane == ei
        raw = jnp.where(pick, scores, raw)
        v = jnp.where(pick, NEG_INF, v)
    gates = raw / jnp.sum(raw, axis=1, keepdims=True) * ROUTED_SCALE
    return jnp.where(lane == n_experts, 1.0, gates)


def _post_kernel(of_ref, os_ref, x_ref, ga_ref, sh_ref, sc_ref, gf_ref, gs_ref, wo_ref, gn_ref,
                 wr_ref, br_ref, x1_ref, h2_ref, gate_ref, *, n_experts):
    o = jnp.concatenate([_rms(of_ref[...], gf_ref[...]), _rms(os_ref[...], gs_ref[...])], axis=1)
    x1 = x_ref[...] + ga_ref[...] * _dot(o.astype(BF16), wo_ref[...])
    x1_ref[...] = x1
    h2 = _rms(x1, gn_ref[...]) * (1.0 + sc_ref[...]) + sh_ref[...]
    h2_ref[...] = h2.astype(BF16)
    scores = _sigmoid(_dot_f32(h2, wr_ref[...]))
    gate_ref[...] = _route(scores + br_ref[...], scores, n_experts)


def _post(o_fox, o_sb, x, mod, g_fox, g_sb, w_out, g_ffn, wr_pad, br_pad, n_experts, tm):
    t, d = x.shape
    per_token = mod.shape[0] != 1
    mrow = (lambda i: i) if per_token else (lambda i: 0)
    mr = tm if per_token else 1
    full = lambda shape: pl.BlockSpec(shape, lambda i: (0, 0))
    tok = lambda w: pl.BlockSpec((tm, w), lambda i: (i, 0))
    modspec = lambda j: pl.BlockSpec((mr, d), lambda i: (mrow(i), j))
    return pl.pallas_call(
        functools.partial(_post_kernel, n_experts=n_experts),
        grid=(t // tm,),
        in_specs=[tok(d // 2), tok(d // 2), tok(d), modspec(2), modspec(3), modspec(4),
                  full((1, d // 2)), full((1, d // 2)), full((d, d)), full((1, d)),
                  full((d, LANES)), full((1, LANES))],
        out_specs=[tok(d), tok(d), tok(LANES)],
        out_shape=[jax.ShapeDtypeStruct((t, d), F32), jax.ShapeDtypeStruct((t, d), BF16),
                   jax.ShapeDtypeStruct((t, LANES), F32)],
        compiler_params=_cparams("arbitrary"),
        name="post_attention",
    )(o_fox, o_sb, x, mod, mod, mod, g_fox, g_sb, w_out, g_ffn, wr_pad, br_pad)


def _moe_kernel(h_ref, gate_ref, wgu_ref, wd_ref, x1_ref, ga_ref, gfin_ref, y_ref, acc_ref):
    e = pl.program_id(1)

    @pl.when(e == 0)
    def _():
        acc_ref[...] = jnp.zeros_like(acc_ref)

    gu = _dot(h_ref[...], wgu_ref[...])
    ff = gu.shape[1] // 2
    gpre = gu[:, :ff]
    act = gpre * _sigmoid(gpre) * gu[:, ff:]
    gates = gate_ref[...]
    lane = lax.broadcasted_iota(jnp.int32, gates.shape, 1)
    g = jnp.sum(jnp.where(lane == e, gates, 0.0), axis=1, keepdims=True)
    acc_ref[...] += _dot((act * g).astype(BF16), wd_ref[...])

    @pl.when(e == pl.num_programs(1) - 1)
    def _():
        x2 = x1_ref[...] + ga_ref[...] * acc_ref[...]
        y_ref[...] = _rms(x2, gfin_ref[...])


def _moe(h2, gates, wgu, wd, x1, mod, g_final, tm):
    t, d = x1.shape
    n_e, _, ff2 = wgu.shape
    per_token = mod.shape[0] != 1
    mrow = (lambda i: i) if per_token else (lambda i: 0)
    mr = tm if per_token else 1
    return pl.pallas_call(
        _moe_kernel,
        grid=(t // tm, n_e),
        in_specs=[pl.BlockSpec((tm, d), lambda i, e: (i, 0)),
                  pl.BlockSpec((tm, LANES), lambda i, e: (i, 0)),
                  pl.BlockSpec((None, d, ff2), lambda i, e: (e, 0, 0)),
                  pl.BlockSpec((None, ff2 // 2, d), lambda i, e: (e, 0, 0)),
                  pl.BlockSpec((tm, d), lambda i, e: (i, 0)),
                  pl.BlockSpec((mr, d), lambda i, e: (mrow(i), 5)),
                  pl.BlockSpec((1, d), lambda i, e: (0, 0))],
        out_specs=pl.BlockSpec((tm, d), lambda i, e: (i, 0)),
        out_shape=jax.ShapeDtypeStruct((t, d), F32),
        scratch_shapes=[pltpu.VMEM((tm, d), F32)],
        compiler_params=_cparams("arbitrary", "arbitrary"),
        name="experts",
    )(h2, gates, wgu, wd, x1, mod, g_final)


def _pad_lanes(a):
    return jnp.pad(a, ((0, 0), (0, LANES - a.shape[1])))


def _tile(n, pref):
    return pref if n % pref == 0 else n


def kernel(x_prompt, x_sample, c_prompt, c_sample, cache_k, cache_v, cache_logf, page_table, w_ada, b_ada, g_attn, w_in, b_f, g_out_fox, g_out_sb, w_out, g_ffn, w_router, b_router, w_gate, w_up, w_down, ws_gate, ws_up, ws_down, g_final):
    depth = w_ada.shape[0]
    assert depth == 1 and x_prompt.shape[0] == 1
    _, seq, d = x_prompt.shape
    dec_b, n_tok, _ = x_sample.shape
    n_heads = d // HEAD_DIM
    h_fox = w_in.shape[2] - 3 * d
    n_experts = w_router.shape[2]
    page = cache_k.shape[2]

    wq = w_in[0, :, :d].astype(BF16)
    wk = w_in[0, :, d:2 * d].astype(BF16)
    wv = w_in[0, :, 2 * d:3 * d].astype(BF16)
    wf_pad = _pad_lanes(w_in[0, :, 3 * d:])
    bf_pad = _pad_lanes(b_f)
    wr_pad = _pad_lanes(w_router[0])
    br_pad = _pad_lanes(b_router)
    wgu = jnp.concatenate([jnp.concatenate([w_gate[0], w_up[0]], axis=2),
                           jnp.concatenate([ws_gate, ws_up], axis=2)], axis=0).astype(BF16)
    wd = jnp.concatenate([w_down[0], ws_down], axis=0).astype(BF16)
    wo = w_out[0].astype(BF16)

    c_all = jnp.concatenate([c_prompt, c_sample], axis=0)
    n_c = c_all.shape[0]
    c_pad = jnp.pad(c_all, ((0, -n_c % 8), (0, 0)))
    mod = _mod(c_pad, w_ada[0], b_ada[0])
    mod_p = mod[:1]
    mod_s = jnp.repeat(mod[1:1 + dec_b], n_tok, axis=0)

    xp = x_prompt.reshape(seq, d)
    xs = x_sample.reshape(dec_b * n_tok, d)
    tm_p = _tile(seq, 512)
    qb, kb, vb, k_p, v_p, lf_p, cum_p = _inproj(xp, mod_p, g_attn, wq, wk, wv, wf_pad, bf_pad, tm_p)
    qs, ks_b, vs_b, k_s, v_s, lf_s, _ = _inproj(xs, mod_s, g_attn, wq, wk, wv, wf_pad, bf_pad,
                                                dec_b * n_tok)

    cum_pairs = cum_p[:, :h_fox].T.reshape(h_fox // 2, 2, seq)
    o_fox_p, o_sb_p = _prompt_attention(qb, kb, vb, cum_pairs, _tile(seq, 256))

    cache_kt = jnp.transpose(cache_k[0], (0, 2, 3, 1)).reshape(-1, d, page)
    cache_vt = jnp.transpose(cache_v[0], (0, 2, 3, 1)).reshape(-1, d, page)
    cache_lft = jnp.transpose(cache_logf[0], (0, 2, 1))
    q3 = qs.reshape(dec_b, n_tok, d)
    r = jnp.arange(2 * n_tok * h_fox)
    row_tok = (r // h_fox) % n_tok
    row_head = (r // (n_tok * h_fox)) * h_fox + r % h_fox
    own = row_head[:, None] == (jnp.arange(d) // HEAD_DIM)[None, :]
    q_blk = jnp.where(own[None], q3[:, row_tok, :], jnp.zeros((), BF16))
    pad_rows = ((0, 0), (0, 16 - n_tok), (0, 0))
    kn = jnp.pad(ks_b.reshape(dec_b, n_tok, d), pad_rows)
    vn = jnp.pad(vs_b.reshape(dec_b, n_tok, d), pad_rows)
    lfn = jnp.transpose(lf_s[:, :h_fox].reshape(dec_b, n_tok, h_fox), (0, 2, 1))
    lfn = jnp.pad(lfn, ((0, 0), (0, 0), (0, LANES - n_tok)))
    o_s = _sample_attention(page_table, q_blk, kn, vn, lfn, cache_kt, cache_vt, cache_lft,
                            n_tok, h_fox)[:, :n_tok].reshape(dec_b * n_tok, d)

    half = d // 2
    x1_p, h2_p, gates_p = _post(o_fox_p, o_sb_p, xp, mod_p, g_out_fox, g_out_sb, wo, g_ffn,
                                wr_pad, br_pad, n_experts, tm_p)
    x1_s, h2_s, gates_s = _post(o_s[:, :half], o_s[:, half:], xs, mod_s, g_out_fox, g_out_sb, wo,
                                g_ffn, wr_pad, br_pad, n_experts, dec_b * n_tok)
    g_fin = g_final.reshape(1, d)
    y_p = _moe(h2_p, gates_p, wgu, wd, x1_p, mod_p, g_fin, _tile(seq, 1024))
    y_s = _moe(h2_s, gates_s, wgu, wd, x1_s, mod_s, g_fin, dec_b * n_tok)

    heads = lambda a, lead: a.reshape(lead + (n_heads, HEAD_DIM))
    return (y_p.reshape(1, seq, d),
            y_s.reshape(dec_b, n_tok, d),
            heads(k_p, (1, 1, seq)), heads(v_p, (1, 1, seq)),
            lf_p[:, :h_fox].reshape(1, 1, seq, h_fox).astype(cache_logf.dtype),
            heads(k_s, (1, dec_b, n_tok)), heads(v_s, (1, dec_b, n_tok)),
            lf_s[:, :h_fox].reshape(1, dec_b, n_tok, h_fox).astype(cache_logf.dtype))
```
